```python
import jax, jax.numpy as jnp
from jax import lax
import numpy as np

D_MODEL = 1024
BATCH = 8
SEQ = 2048
DEPTH = 1
DEC_BATCH = 128
DEC_SEQ = 4
PAST_LEN = 16384
PAGE_SIZE = 128

N_META = 16
POOL_WINDOWS = (2, 4, 8, 16)
N_POOL_GROUPS = len(POOL_WINDOWS)
POOL_W = D_MODEL // 2
POOL_GC = POOL_W // N_POOL_GROUPS
POOL_BUF = max(POOL_WINDOWS) - 1
CONV_W = D_MODEL - POOL_W
CONV_HEADS = 4
CONV_K = 3
CONV_BUF = CONV_K - 1
MIX_W = POOL_W + CONV_W
IN_W = POOL_W + 3 * CONV_W
D_FF = 4 * D_MODEL
EPS = 1e-6

kernel_name = "hymba_pool_shortconv_decode_step"


def rmsnorm(x, g):
    xf = x.astype(jnp.float32)
    y = xf * lax.rsqrt(jnp.mean(xf * xf, axis=-1, keepdims=True) + EPS)
    return (y * g.astype(jnp.float32)).astype(x.dtype)


def pool_mixer(u, u_past, p0, pool_w, pool_scale):
    b, t, _ = u.shape
    ext = jnp.concatenate([u_past, u], axis=1)
    cs = jnp.cumsum(ext.astype(jnp.float32), axis=1)
    cs = jnp.pad(cs, ((0, 0), (1, 0), (0, 0)))
    hi = cs[:, POOL_BUF + 1:]
    pos = p0 + jnp.arange(t, dtype=jnp.int32)
    means = []
    for g, w in enumerate(POOL_WINDOWS):
        sl = slice(g * POOL_GC, (g + 1) * POOL_GC)
        lo = cs[:, POOL_BUF + 1 - w: POOL_BUF + 1 - w + t, sl]
        cnt = jnp.minimum(w, pos + 1).astype(jnp.float32)
        means.append((hi[..., sl] - lo) / cnt[None, :, None])
    mean = jnp.concatenate(means, axis=-1)
    d = (mean - u.astype(jnp.float32)).astype(u.dtype).reshape(b, t, N_POOL_GROUPS, POOL_GC)
    out = jnp.einsum('btgc,gcd->btgd', d, pool_w).reshape(b, t, POOL_W)
    return out * pool_scale, ext[:, -POOL_BUF:]


def conv_mixer(bg, cg, h, z_past, conv_w):
    t = h.shape[1]
    z = cg * h
    ext = jnp.concatenate([z_past, z], axis=1)
    y = ext[:, 0:t] * conv_w[0] + ext[:, 1:1 + t] * conv_w[1] + ext[:, 2:2 + t] * conv_w[2]
    return bg * y, ext[:, -CONV_BUF:]


def layer(x, s_pool, s_conv, p0, norm1_g, w_in, pool_w, pool_scale, conv_w, w_out,
          norm2_g, w1, w2):
    hn = rmsnorm(x, norm1_g)
    proj = jnp.einsum('btd,de->bte', hn, w_in)
    u = proj[..., :POOL_W]
    bg = proj[..., POOL_W:POOL_W + CONV_W]
    cg = proj[..., POOL_W + CONV_W:POOL_W + 2 * CONV_W]
    hc = proj[..., POOL_W + 2 * CONV_W:]
    ya, new_pool = pool_mixer(u, s_pool, p0, pool_w, pool_scale)
    yb, new_conv = conv_mixer(bg, cg, hc, s_conv, conv_w)
    mix = jnp.concatenate([ya, yb], axis=-1)
    x = x + jnp.einsum('bte,ed->btd', mix, w_out)
    hn = rmsnorm(x, norm2_g)
    a = jax.nn.relu(jnp.einsum('btd,df->btf', hn, w1))
    x = x + jnp.einsum('btf,fd->btd', a * a, w2)
    return x, new_pool, new_conv


def setup_inputs(seed: int = 0) -> dict:
    key = jax.random.key(seed)
    ks = jax.random.split(key, 16)
    f32 = jnp.float32
    n = lambda k, s, sc: jax.random.normal(k, s, f32) * sc
    return {
        "x_prompt": n(ks[0], (BATCH, SEQ, D_MODEL), 1.0),
        "x_sample": n(ks[1], (DEC_BATCH, DEC_SEQ, D_MODEL), 1.0),
        "state_pool": n(ks[2], (DEPTH, DEC_BATCH, POOL_BUF, POOL_W), 1.0),
        "state_conv": n(ks[3], (DEPTH, DEC_BATCH, CONV_BUF, CONV_W), 1.0),
        "meta_tokens": n(ks[4], (N_META, D_MODEL), 1.0),
        "norm1_g": 1.0 + n(ks[5], (DEPTH, D_MODEL), 0.05),
        "w_in": n(ks[6], (DEPTH, D_MODEL, IN_W), D_MODEL ** -0.5),
        "pool_w": n(ks[7], (DEPTH, N_POOL_GROUPS, POOL_GC, POOL_GC), POOL_GC ** -0.5),
        "pool_scale": 1.0 + n(ks[8], (DEPTH, POOL_W), 0.1),
        "conv_w": n(ks[9], (DEPTH, CONV_K, CONV_W), CONV_K ** -0.5),
        "w_out": n(ks[10], (DEPTH, MIX_W, D_MODEL), MIX_W ** -0.5),
        "norm2_g": 1.0 + n(ks[11], (DEPTH, D_MODEL), 0.05),
        "w1": n(ks[12], (DEPTH, D_MODEL, D_FF), D_MODEL ** -0.5),
        "w2": n(ks[13], (DEPTH, D_FF, D_MODEL), D_FF ** -0.5),
        "final_g": 1.0 + n(ks[14], (D_MODEL,), 0.05),
    }


def reference(x_prompt, x_sample, state_pool, state_conv, meta_tokens, norm1_g, w_in,
              pool_w, pool_scale, conv_w, w_out, norm2_g, w1, w2, final_g):
    b = x_prompt.shape[0]
    meta = jnp.broadcast_to(meta_tokens.astype(x_prompt.dtype)[None], (b, N_META, D_MODEL))
    xp = jnp.concatenate([meta, x_prompt], axis=1)
    xs = x_sample
    zp_pool = jnp.zeros((b, POOL_BUF, POOL_W), x_prompt.dtype)
    zp_conv = jnp.zeros((b, CONV_BUF, CONV_W), x_prompt.dtype)
    pp, cp, ps, cs = [], [], [], []
    for l in range(DEPTH):
        params = (norm1_g[l], w_in[l], pool_w[l], pool_scale[l], conv_w[l], w_out[l],
                  norm2_g[l], w1[l], w2[l])
        xp, np_pool, np_conv = layer(xp, zp_pool, zp_conv, 0, *params)
        xs, ns_pool, ns_conv = layer(xs, state_pool[l], state_conv[l], PAST_LEN, *params)
        pp.append(np_pool); cp.append(np_conv); ps.append(ns_pool); cs.append(ns_conv)
    y_prompt = rmsnorm(xp, final_g)[:, N_META:]
    y_sample = rmsnorm(xs, final_g)
    new_pool_prompt = jnp.stack(pp, axis=0)
    new_conv_prompt = jnp.stack(cp, axis=0)
    new_pool_sample = jnp.stack(ps, axis=0)
    new_conv_sample = jnp.stack(cs, axis=0)
    return (y_prompt, y_sample, new_pool_prompt, new_conv_prompt, new_pool_sample, new_conv_sample)
```

```python
import functools

import jax
import jax.numpy as jnp
from jax import lax
from jax.experimental import pallas as pl
from jax.experimental.pallas import tpu as pltpu

D_MODEL = 1024
N_META = 16
POOL_WINDOWS = (2, 4, 8, 16)
N_POOL_GROUPS = len(POOL_WINDOWS)
POOL_W = D_MODEL // 2
POOL_GC = POOL_W // N_POOL_GROUPS
POOL_BUF = max(POOL_WINDOWS) - 1
CONV_W = D_MODEL - POOL_W
CONV_K = 3
CONV_BUF = CONV_K - 1
IN_W = POOL_W + 3 * CONV_W
D_FF = 4 * D_MODEL
EPS = 1e-6
PAST_LEN = 16384

SUBLANES = 8
LANES = 128
POOL_HIST = 16
CONV_HIST = 8
TILE_T = 512
FF_CHUNK = 1024
VMEM_LIMIT_BYTES = 56 * 1024 * 1024

assert POOL_GC == LANES and POOL_BUF <= POOL_HIST and CONV_BUF <= CONV_HIST
assert N_META + 1 >= max(POOL_WINDOWS) and PAST_LEN + 1 >= max(POOL_WINDOWS)


def _rmsnorm(x, g):
    y = x * lax.rsqrt(jnp.mean(x * x, axis=-1, keepdims=True) + EPS)
    return y * g


def _dot(a, b):
    return jnp.dot(a.astype(jnp.bfloat16), b, preferred_element_type=jnp.float32)


def _in_proj(x, g1_ref, win_ref):
    return _dot(_rmsnorm(x, g1_ref[...]), win_ref[...])


def _pool_out(d, pw_ref, ps_ref):
    outs = [
        _dot(d[:, g * POOL_GC:(g + 1) * POOL_GC], pw_ref[g])
        for g in range(N_POOL_GROUPS)
    ]
    return jnp.concatenate(outs, axis=-1) * ps_ref[...]


def _post_mixer(x, ya, yb, wout_ref, g2_ref, w1_ref, w2_ref, gf_ref):
    mix = jnp.concatenate([ya, yb], axis=-1)
    x = x + _dot(mix, wout_ref[...])
    hn = _rmsnorm(x, g2_ref[...]).astype(jnp.bfloat16)
    acc = None
    for c in range(D_FF // FF_CHUNK):
        cols = slice(c * FF_CHUNK, (c + 1) * FF_CHUNK)
        a = jnp.maximum(_dot(hn, w1_ref[:, cols]), 0.0)
        part = _dot(a * a, w2_ref[cols, :])
        acc = part if acc is None else acc + part
    x = x + acc
    return _rmsnorm(x, gf_ref[...])


def _meta_kernel(meta_ref, g1_ref, win_ref, u_ref, z_ref):
    proj = _in_proj(meta_ref[...], g1_ref, win_ref)
    u_ref[...] = proj[:, :POOL_W]
    z = proj[:, POOL_W + CONV_W:POOL_W + 2 * CONV_W] * proj[:, POOL_W + 2 * CONV_W:]
    z_ref[...] = z[N_META - CONV_HIST:]


def _prompt_kernel(x_ref, u0_ref, z0_ref, g1_ref, win_ref, pw_ref, ps_ref, cw_ref,
                   wout_ref, g2_ref, w1_ref, w2_ref, gf_ref,
                   y_ref, npool_ref, nconv_ref, uhist_ref, zhist_ref):
    t = pl.program_id(1)

    @pl.when(t == 0)
    def _():
        uhist_ref[...] = u0_ref[...]
        zhist_ref[...] = z0_ref[...]

    x = x_ref[0]
    proj = _in_proj(x, g1_ref, win_ref)
    u = proj[:, :POOL_W]
    bg = proj[:, POOL_W:POOL_W + CONV_W]
    z = proj[:, POOL_W + CONV_W:POOL_W + 2 * CONV_W] * proj[:, POOL_W + 2 * CONV_W:]

    s = jnp.concatenate([uhist_ref[...], u], axis=0)
    means = []
    for g, w in enumerate(POOL_WINDOWS):
        s = s + pltpu.roll(s, w // 2, axis=0)
        means.append(s[POOL_HIST:, :POOL_GC] * (1.0 / w))
        s = s[:, POOL_GC:]
    d = jnp.concatenate(means, axis=-1) - u
    ya = _pool_out(d, pw_ref, ps_ref)

    zext = jnp.concatenate([zhist_ref[...], z], axis=0)
    conv = (pltpu.roll(zext, 2, axis=0) * cw_ref[0:1, :]
            + pltpu.roll(zext, 1, axis=0) * cw_ref[1:2, :]
            + zext * cw_ref[2:3, :])
    yb = bg * conv[CONV_HIST:]

    y_ref[0] = _post_mixer(x, ya, yb, wout_ref, g2_ref, w1_ref, w2_ref, gf_ref)

    u_tail = u[TILE_T - POOL_HIST:]
    z_tail = z[TILE_T - CONV_HIST:]
    uhist_ref[...] = u_tail
    zhist_ref[...] = z_tail

    @pl.when(t == pl.num_programs(1) - 1)
    def _():
        npool_ref[0, 0] = pltpu.roll(u_tail, POOL_BUF, axis=0)[:POOL_BUF]
        nconv_ref[0, 0] = pltpu.roll(z_tail, CONV_BUF, axis=0)[:CONV_BUF]


def _sample_kernel(x_ref, spool_ref, sconv_ref, g1_ref, win_ref, pw_ref, ps_ref, cw_ref,
                   wout_ref, g2_ref, w1_ref, w2_ref, gf_ref,
                   y_ref, npool_ref, nconv_ref):
    n_t, n_b, _ = x_ref.shape
    x = x_ref[...].reshape(n_t * n_b, D_MODEL)
    proj = _in_proj(x, g1_ref, win_ref)
    u = proj[:, :POOL_W]
    bg = proj[:, POOL_W:POOL_W + CONV_W]
    z = proj[:, POOL_W + CONV_W:POOL_W + 2 * CONV_W] * proj[:, POOL_W + 2 * CONV_W:]

    def rows(a, i):
        return a[i * n_b:(i + 1) * n_b]

    ext = [spool_ref[i] for i in range(POOL_BUF)] + [rows(u, i) for i in range(n_t)]
    d_rows = []
    for i in range(n_t):
        means = []
        for g, w in enumerate(POOL_WINDOWS):
            lanes = slice(g * POOL_GC, (g + 1) * POOL_GC)
            acc = ext[POOL_BUF + i][:, lanes]
            for j in range(1, w):
                acc = acc + ext[POOL_BUF + i - j][:, lanes]
            means.append(acc * (1.0 / w))
        d_rows.append(jnp.concatenate(means, axis=-1) - rows(u, i))
    ya = _pool_out(jnp.concatenate(d_rows, axis=0), pw_ref, ps_ref)

    zext = [sconv_ref[i] for i in range(CONV_BUF)] + [rows(z, i) for i in range(n_t)]
    conv = [
        zext[i] * cw_ref[0:1, :] + zext[i + 1] * cw_ref[1:2, :] + zext[i + 2] * cw_ref[2:3, :]
        for i in range(n_t)
    ]
    yb = bg * jnp.concatenate(conv, axis=0)

    y = _post_mixer(x, ya, yb, wout_ref, g2_ref, w1_ref, w2_ref, gf_ref)
    y_ref[...] = y.reshape(n_t, n_b, D_MODEL)

    for i in range(POOL_BUF):
        npool_ref[i] = ext[n_t + i]
    for i in range(CONV_BUF):
        nconv_ref[i] = zext[n_t + i]


def _resident(shape):
    zeros = (0,) * len(shape)
    return pl.BlockSpec(shape, lambda *_: zeros, pipeline_mode=pl.Buffered(1))


def kernel(x_prompt, x_sample, state_pool, state_conv, meta_tokens, norm1_g, w_in, pool_w,
           pool_scale, conv_w, w_out, norm2_g, w1, w2, final_g):
    f32, bf16 = jnp.float32, jnp.bfloat16
    batch, seq, _ = x_prompt.shape
    dec_batch, dec_seq, _ = x_sample.shape
    assert norm1_g.shape[0] == 1 and seq % TILE_T == 0 and seq >= POOL_HIST

    g1 = norm1_g[0].reshape(1, D_MODEL)
    g2 = norm2_g[0].reshape(1, D_MODEL)
    gf = final_g.reshape(1, D_MODEL)
    ps = pool_scale[0].reshape(1, POOL_W)
    cw = conv_w[0]
    win = w_in[0].astype(bf16)
    pw = pool_w[0].astype(bf16)
    wout = w_out[0].astype(bf16)
    w1b = w1[0].astype(bf16)
    w2b = w2[0].astype(bf16)

    weights = (g1, win, pw, ps, cw, wout, g2, w1b, w2b, gf)
    weight_specs = [_resident(w.shape) for w in weights]

    u_meta, z_meta = pl.pallas_call(
        _meta_kernel,
        out_shape=(jax.ShapeDtypeStruct((POOL_HIST, POOL_W), f32),
                   jax.ShapeDtypeStruct((CONV_HIST, CONV_W), f32)),
        compiler_params=pltpu.CompilerParams(vmem_limit_bytes=VMEM_LIMIT_BYTES),
        name="meta_in_proj",
    )(meta_tokens, g1, win)

    n_t = seq // TILE_T
    y_prompt, npool_p, nconv_p = pl.pallas_call(
        _prompt_kernel,
        grid=(batch, n_t),
        in_specs=[pl.BlockSpec((1, TILE_T, D_MODEL), lambda b, t: (b, t, 0)),
                  _resident((POOL_HIST, POOL_W)),
                  _resident((CONV_HIST, CONV_W))] + weight_specs,
        out_specs=(pl.BlockSpec((1, TILE_T, D_MODEL), lambda b, t: (b, t, 0)),
                   pl.BlockSpec((1, 1, POOL_BUF, POOL_W), lambda b, t: (0, b, 0, 0)),
                   pl.BlockSpec((1, 1, CONV_BUF, CONV_W), lambda b, t: (0, b, 0, 0))),
        out_shape=(jax.ShapeDtypeStruct((batch, seq, D_MODEL), f32),
                   jax.ShapeDtypeStruct((1, batch, POOL_BUF, POOL_W), f32),
                   jax.ShapeDtypeStruct((1, batch, CONV_BUF, CONV_W), f32)),
        scratch_shapes=[pltpu.VMEM((POOL_HIST, POOL_W), f32),
                        pltpu.VMEM((CONV_HIST, CONV_W), f32)],
        compiler_params=pltpu.CompilerParams(
            dimension_semantics=("arbitrary", "arbitrary"),
            vmem_limit_bytes=VMEM_LIMIT_BYTES),
        name="prompt_layer",
    )(x_prompt, u_meta, z_meta, *weights)

    x_tm = jnp.transpose(x_sample, (1, 0, 2))
    spool_tm = jnp.transpose(state_pool[0], (1, 0, 2))
    sconv_tm = jnp.transpose(state_conv[0], (1, 0, 2))
    y_tm, npool_tm, nconv_tm = pl.pallas_call(
        _sample_kernel,
        out_shape=(jax.ShapeDtypeStruct((dec_seq, dec_batch, D_MODEL), f32),
                   jax.ShapeDtypeStruct((POOL_BUF, dec_batch, POOL_W), f32),
                   jax.ShapeDtypeStruct((CONV_BUF, dec_batch, CONV_W), f32)),
        compiler_params=pltpu.CompilerParams(vmem_limit_bytes=VMEM_LIMIT_BYTES),
        name="sample_layer",
    )(x_tm, spool_tm, sconv_tm, *weights)

    y_sample = jnp.transpose(y_tm, (1, 0, 2))
    npool_s = jnp.transpose(npool_tm, (1, 0, 2))[None]
    nconv_s = jnp.transpose(nconv_tm, (1, 0, 2))[None]
    return (y_prompt, y_sample, npool_p, nconv_p, npool_s, nconv_s)
```

```python
import functools

import jax
import jax.numpy as jnp
from jax import lax
from jax.experimental import pallas as pl
from jax.experimental.pallas import tpu as pltpu

D_MODEL = 1024
N_META = 16
POOL_WINDOWS = (2, 4, 8, 16)
N_POOL_GROUPS = len(POOL_WINDOWS)
POOL_W = D_MODEL // 2
POOL_GC = POOL_W // N_POOL_GROUPS
POOL_BUF = max(POOL_WINDOWS) - 1
CONV_W = D_MODEL - POOL_W
CONV_K = 3
CONV_BUF = CONV_K - 1
IN_W = POOL_W + 3 * CONV_W
D_FF = 4 * D_MODEL
EPS = 1e-6
PAST_LEN = 16384

SUBLANES = 8
LANES = 128
POOL_HIST = 16
CONV_HIST = 8
TILE_T = 512
FF_CHUNK = 1024
IN_CHUNK = 512
N_IN_CHUNKS = IN_W // IN_CHUNK
S_FF_CHUNK = 512
N_FF_CHUNKS = D_FF // S_FF_CHUNK
MIX_STEP = N_IN_CHUNKS
VMEM_LIMIT_BYTES = 56 * 1024 * 1024

assert POOL_GC == LANES and POOL_BUF <= POOL_HIST and CONV_BUF <= CONV_HIST
assert IN_CHUNK == POOL_W == CONV_W
assert N_META + 1 >= max(POOL_WINDOWS) and PAST_LEN + 1 >= max(POOL_WINDOWS)


def _rmsnorm(x, g):
    y = x * lax.rsqrt(jnp.mean(x * x, axis=-1, keepdims=True) + EPS)
    return y * g


def _dot(a, b):
    return jnp.dot(a.astype(jnp.bfloat16), b.astype(jnp.bfloat16),
                   preferred_element_type=jnp.float32)


def _lanes(g):
    return slice(g * LANES, (g + 1) * LANES)


def _pool_out(d, pw_ref, ps_ref):
    outs = [_dot(d[:, _lanes(g)], pw_ref[g]) for g in range(N_POOL_GROUPS)]
    return jnp.concatenate(outs, axis=-1) * ps_ref[...]


def _post_mixer(x, ya, yb, wout_ref, g2_ref, w1_ref, w2_ref, gf_ref):
    mix = jnp.concatenate([ya, yb], axis=-1)
    x = x + _dot(mix, wout_ref[...])
    hn = _rmsnorm(x, g2_ref[...]).astype(jnp.bfloat16)
    acc = None
    for c in range(D_FF // FF_CHUNK):
        cols = slice(c * FF_CHUNK, (c + 1) * FF_CHUNK)
        a = jnp.maximum(_dot(hn, w1_ref[:, cols]), 0.0)
        part = _dot(a * a, w2_ref[cols, :])
        acc = part if acc is None else acc + part
    x = x + acc
    return _rmsnorm(x, gf_ref[...])


def _prompt_kernel(x_ref, u0_ref, z0_ref, g1_ref, win_ref, pw_ref, ps_ref, cw_ref,
                   wout_ref, g2_ref, w1_ref, w2_ref, gf_ref,
                   y_ref, npool_ref, nconv_ref, uhist_ref, zhist_ref):
    t = pl.program_id(1)

    @pl.when(t == 0)
    def _():
        uhist_ref[...] = u0_ref[...]
        zhist_ref[...] = z0_ref[...]

    x = x_ref[0]
    proj = _dot(_rmsnorm(x, g1_ref[...]), win_ref[...])
    u = proj[:, :POOL_W]
    bg = proj[:, POOL_W:POOL_W + CONV_W]
    z = proj[:, POOL_W + CONV_W:POOL_W + 2 * CONV_W] * proj[:, POOL_W + 2 * CONV_W:]

    s = jnp.concatenate([uhist_ref[...], u], axis=0)
    means = []
    for g, w in enumerate(POOL_WINDOWS):
        s = s + pltpu.roll(s, w // 2, axis=0)
        means.append(s[POOL_HIST:, :POOL_GC] * (1.0 / w))
        s = s[:, POOL_GC:]
    d = jnp.concatenate(means, axis=-1) - u
    ya = _pool_out(d, pw_ref, ps_ref)

    zext = jnp.concatenate([zhist_ref[...], z], axis=0)
    conv = (pltpu.roll(zext, 2, axis=0) * cw_ref[0:1, :]
            + pltpu.roll(zext, 1, axis=0) * cw_ref[1:2, :]
            + zext * cw_ref[2:3, :])
    yb = bg * conv[CONV_HIST:]

    y_ref[0] = _post_mixer(x, ya, yb, wout_ref, g2_ref, w1_ref, w2_ref, gf_ref)

    u_tail = u[TILE_T - POOL_HIST:]
    z_tail = z[TILE_T - CONV_HIST:]
    uhist_ref[...] = u_tail
    zhist_ref[...] = z_tail

    @pl.when(t == pl.num_programs(1) - 1)
    def _():
        npool_ref[0, 0] = pltpu.roll(u_tail, POOL_BUF, axis=0)[:POOL_BUF]
        nconv_ref[0, 0] = pltpu.roll(z_tail, CONV_BUF, axis=0)[:CONV_BUF]


def _sample_mixers(n_seq, n_new, proj_ref, sp_ref, sc_ref, pw_ref, ps_ref, cw_ref,
                   npool_ref, nconv_ref, hist_ref, new_ref, res_ref):
    n_rows = n_seq * n_new

    def every(j, per_seq):
        return pl.ds(j, n_seq, stride=per_seq)

    d_groups = []
    for g, w in enumerate(POOL_WINDOWS):
        hist_ref[...] = sp_ref[:, _lanes(g)]
        new_ref[...] = proj_ref[0, :n_rows, _lanes(g)]
        ext = [hist_ref[every(j, POOL_BUF), :] for j in range(POOL_BUF)]
        ext += [new_ref[every(j, n_new), :] for j in range(n_new)]
        for i in range(n_new):
            acc = ext[POOL_BUF + i]
            for j in range(1, w):
                acc = acc + ext[POOL_BUF + i - j]
            res_ref[every(i, n_new), :] = acc * (1.0 / w) - ext[POOL_BUF + i]
        d_groups.append(res_ref[...])
        for i in range(POOL_BUF):
            hist_ref[every(i, POOL_BUF), :] = ext[n_new + i]
        npool_ref[:, _lanes(g)] = hist_ref[...]
    ya = _pool_out(jnp.concatenate(d_groups, axis=-1), pw_ref, ps_ref)

    conv_groups = []
    n_sc = n_seq * CONV_BUF
    for g in range(CONV_W // LANES):
        hist_ref[:n_sc, :] = sc_ref[:, _lanes(g)]
        new_ref[...] = proj_ref[2, :n_rows, _lanes(g)] * proj_ref[3, :n_rows, _lanes(g)]
        zext = [hist_ref[every(j, CONV_BUF), :] for j in range(CONV_BUF)]
        zext += [new_ref[every(j, n_new), :] for j in range(n_new)]
        for i in range(n_new):
            res_ref[every(i, n_new), :] = (
                zext[i] * cw_ref[0:1, _lanes(g)]
                + zext[i + 1] * cw_ref[1:2, _lanes(g)]
                + zext[i + 2] * cw_ref[2:3, _lanes(g)])
        conv_groups.append(res_ref[...])
        for i in range(CONV_BUF):
            hist_ref[every(i, CONV_BUF), :] = zext[n_new + i]
        nconv_ref[:, _lanes(g)] = hist_ref[:n_sc, :]
    yb = proj_ref[1, :n_rows, :] * jnp.concatenate(conv_groups, axis=-1)
    return ya, yb


def _sample_kernel(n_seq, n_new,
                   xs_ref, meta_ref, sp_ref, sc_ref, g1_ref, win_ref, pw_ref, ps_ref, cw_ref,
                   wout_ref, g2_ref, w1_ref, w2_ref, gf_ref,
                   y_ref, npool_ref, nconv_ref, u0_ref, z0_ref,
                   hn_ref, proj_ref, hn2_ref, hist_ref, new_ref, res_ref):
    i = pl.program_id(0)
    n_rows = n_seq * n_new

    @pl.when(i == 0)
    def _():
        hn_ref[:n_rows, :] = _rmsnorm(xs_ref[...], g1_ref[...]).astype(jnp.bfloat16)
        hn_ref[n_rows:, :] = _rmsnorm(meta_ref[...], g1_ref[...]).astype(jnp.bfloat16)

    @pl.when(i < N_IN_CHUNKS)
    def _():
        proj_ref[i] = _dot(hn_ref[...], win_ref[...])

    @pl.when(i == MIX_STEP)
    def _():
        u0_ref[...] = proj_ref[0, n_rows:, :]
        z_meta = proj_ref[2, n_rows:, :] * proj_ref[3, n_rows:, :]
        z0_ref[...] = z_meta[N_META - CONV_HIST:]

        ya, yb = _sample_mixers(n_seq, n_new, proj_ref, sp_ref, sc_ref, pw_ref, ps_ref,
                                cw_ref, npool_ref, nconv_ref, hist_ref, new_ref, res_ref)
        mix = jnp.concatenate([ya, yb], axis=-1)
        x = xs_ref[...] + _dot(mix, wout_ref[...])
        y_ref[...] = x
        hn2_ref[...] = _rmsnorm(x, g2_ref[...]).astype(jnp.bfloat16)

    @pl.when(i > MIX_STEP)
    def _():
        a = jnp.maximum(_dot(hn2_ref[...], w1_ref[...]), 0.0)
        y_ref[...] += _dot(a * a, w2_ref[...])

    @pl.when(i == pl.num_programs(0) - 1)
    def _():
        y_ref[...] = _rmsnorm(y_ref[...], gf_ref[...])


def _resident(shape):
    zeros = (0,) * len(shape)
    return pl.BlockSpec(shape, lambda *_: zeros, pipeline_mode=pl.Buffered(1))


def kernel(x_prompt, x_sample, state_pool, state_conv, meta_tokens, norm1_g, w_in, pool_w,
           pool_scale, conv_w, w_out, norm2_g, w1, w2, final_g):
    f32, bf16 = jnp.float32, jnp.bfloat16
    batch, seq, _ = x_prompt.shape
    n_seq, n_new, _ = x_sample.shape
    n_rows = n_seq * n_new
    assert norm1_g.shape[0] == 1 and seq % TILE_T == 0 and seq >= POOL_HIST
    assert n_seq % SUBLANES == 0 and meta_tokens.shape[0] == N_META

    g1 = norm1_g[0].reshape(1, D_MODEL)
    g2 = norm2_g[0].reshape(1, D_MODEL)
    gf = final_g.reshape(1, D_MODEL)
    ps = pool_scale[0].reshape(1, POOL_W)
    cw = conv_w[0]

    xs = x_sample.reshape(n_rows, D_MODEL)
    sp = state_pool[0].reshape(n_seq * POOL_BUF, POOL_W)
    sc = state_conv[0].reshape(n_seq * CONV_BUF, CONV_W)

    def ff_chunk(i):
        return jnp.clip(i - (MIX_STEP + 1), 0, N_FF_CHUNKS - 1)

    y_s, npool_s, nconv_s, u_meta, z_meta = pl.pallas_call(
        functools.partial(_sample_kernel, n_seq, n_new),
        grid=(N_IN_CHUNKS + 1 + N_FF_CHUNKS,),
        in_specs=[_resident(xs.shape), _resident(meta_tokens.shape),
                  _resident(sp.shape), _resident(sc.shape), _resident(g1.shape),
                  pl.BlockSpec((D_MODEL, IN_CHUNK),
                               lambda i: (0, jnp.minimum(i, N_IN_CHUNKS - 1))),
                  _resident(pool_w.shape[1:]), _resident(ps.shape), _resident(cw.shape),
                  _resident(w_out.shape[1:]), _resident(g2.shape),
                  pl.BlockSpec((D_MODEL, S_FF_CHUNK), lambda i: (0, ff_chunk(i))),
                  pl.BlockSpec((S_FF_CHUNK, D_MODEL), lambda i: (ff_chunk(i), 0)),
                  _resident(gf.shape)],
        out_specs=(pl.BlockSpec((n_rows, D_MODEL), lambda i: (0, 0)),
                   pl.BlockSpec(sp.shape, lambda i: (0, 0)),
                   pl.BlockSpec(sc.shape, lambda i: (0, 0)),
                   pl.BlockSpec((POOL_HIST, POOL_W), lambda i: (0, 0)),
                   pl.BlockSpec((CONV_HIST, CONV_W), lambda i: (0, 0))),
        out_shape=(jax.ShapeDtypeStruct((n_rows, D_MODEL), f32),
                   jax.ShapeDtypeStruct(sp.shape, f32),
                   jax.ShapeDtypeStruct(sc.shape, f32),
                   jax.ShapeDtypeStruct((POOL_HIST, POOL_W), f32),
                   jax.ShapeDtypeStruct((CONV_HIST, CONV_W), f32)),
        scratch_shapes=[pltpu.VMEM((n_rows + N_META, D_MODEL), bf16),
                        pltpu.VMEM((N_IN_CHUNKS, n_rows + N_META, IN_CHUNK), f32),
                        pltpu.VMEM((n_rows, D_MODEL), bf16),
                        pltpu.VMEM((n_seq * POOL_BUF, LANES), f32),
                        pltpu.VMEM((n_rows, LANES), f32),
                        pltpu.VMEM((n_rows, LANES), f32)],
        compiler_params=pltpu.CompilerParams(
            dimension_semantics=("arbitrary",), vmem_limit_bytes=VMEM_LIMIT_BYTES),
        name="sample_layer",
    )(xs, meta_tokens, sp, sc, g1, w_in[0], pool_w[0], ps, cw, w_out[0], g2, w1[0], w2[0], gf)

    win = w_in[0].astype(bf16)
    pw = pool_w[0].astype(bf16)
    wout = w_out[0].astype(bf16)
    w1b = w1[0].astype(bf16)
    w2b = w2[0].astype(bf16)
    weights = (g1, win, pw, ps, cw, wout, g2, w1b, w2b, gf)

    n_t = seq // TILE_T
    y_prompt, npool_p, nconv_p = pl.pallas_call(
        _prompt_kernel,
        grid=(batch, n_t),
        in_specs=[pl.BlockSpec((1, TILE_T, D_MODEL), lambda b, t: (b, t, 0)),
                  _resident((POOL_HIST, POOL_W)),
                  _resident((CONV_HIST, CONV_W))] + [_resident(w.shape) for w in weights],
        out_specs=(pl.BlockSpec((1, TILE_T, D_MODEL), lambda b, t: (b, t, 0)),
                   pl.BlockSpec((1, 1, POOL_BUF, POOL_W), lambda b, t: (0, b, 0, 0)),
                   pl.BlockSpec((1, 1, CONV_BUF, CONV_W), lambda b, t: (0, b, 0, 0))),
        out_shape=(jax.ShapeDtypeStruct((batch, seq, D_MODEL), f32),
                   jax.ShapeDtypeStruct((1, batch, POOL_BUF, POOL_W), f32),
                   jax.ShapeDtypeStruct((1, batch, CONV_BUF, CONV_W), f32)),
        scratch_shapes=[pltpu.VMEM((POOL_HIST, POOL_W), f32),
                        pltpu.VMEM((CONV_HIST, CONV_W), f32)],
        compiler_params=pltpu.CompilerParams(
            dimension_semantics=("arbitrary", "arbitrary"),
            vmem_limit_bytes=VMEM_LIMIT_BYTES),
        name="prompt_layer",
    )(x_prompt, u_meta, z_meta, *weights)

    return (y_prompt,
            y_s.reshape(n_seq, n_new, D_MODEL),
            npool_p,
            nconv_p,
            npool_s.reshape(1, n_seq, POOL_BUF, POOL_W),
            nconv_s.reshape(1, n_seq, CONV_BUF, CONV_W))
```

```python
import jax
import jax.numpy as jnp
from jax import lax
from jax.experimental import pallas as pl
from jax.experimental.pallas import tpu as pltpu

D_MODEL = 1024
N_META = 16
POOL_WINDOWS = (2, 4, 8, 16)
N_POOL_GROUPS = len(POOL_WINDOWS)
POOL_W = D_MODEL // 2
POOL_GC = POOL_W // N_POOL_GROUPS
POOL_BUF = max(POOL_WINDOWS) - 1
CONV_W = D_MODEL - POOL_W
CONV_K = 3
CONV_BUF = CONV_K - 1
IN_W = POOL_W + 3 * CONV_W
D_FF = 4 * D_MODEL
EPS = 1e-6
PAST_LEN = 16384

SUBLANES = 8
LANES = 128
POOL_HIST = 16
CONV_HIST = 8
TILE_T = 512
FF_CHUNK = 1024
IN_CHUNK = 512
N_IN_CHUNKS = IN_W // IN_CHUNK
S_FF_CHUNK = 512
N_FF_CHUNKS = D_FF // S_FF_CHUNK
MIX_STEP = N_IN_CHUNKS
VMEM_LIMIT_BYTES = 56 * 1024 * 1024

assert POOL_GC == LANES and POOL_BUF <= POOL_HIST and CONV_BUF <= CONV_HIST
assert IN_CHUNK == POOL_W == CONV_W
assert N_META + 1 >= max(POOL_WINDOWS) and PAST_LEN + 1 >= max(POOL_WINDOWS)


def _rmsnorm(x, g):
    y = x * lax.rsqrt(jnp.mean(x * x, axis=-1, keepdims=True) + EPS)
    return y * g


def _dot(a, b):
    return jnp.dot(a.astype(jnp.bfloat16), b.astype(jnp.bfloat16),
                   preferred_element_type=jnp.float32)


def _lanes(g):
    return slice(g * LANES, (g + 1) * LANES)


def _pool_out(d, pw_ref, ps_ref):
    outs = [_dot(d[:, _lanes(g)], pw_ref[g]) for g in range(N_POOL_GROUPS)]
    return jnp.concatenate(outs, axis=-1) * ps_ref[...]


def _post_mixer(x, ya, yb, wout_ref, g2_ref, w1_ref, w2_ref, gf_ref):
    mix = jnp.concatenate([ya, yb], axis=-1)
    x = x + _dot(mix, wout_ref[...])
    hn = _rmsnorm(x, g2_ref[...]).astype(jnp.bfloat16)
    acc = None
    for c in range(D_FF // FF_CHUNK):
        cols = slice(c * FF_CHUNK, (c + 1) * FF_CHUNK)
        a = jnp.maximum(_dot(hn, w1_ref[:, cols]), 0.0)
        part = _dot(a * a, w2_ref[cols, :])
        acc = part if acc is None else acc + part
    x = x + acc
    return _rmsnorm(x, gf_ref[...])


def _prompt_kernel(x_ref, u0_ref, z0_ref, g1_ref, win_ref, pw_ref, ps_ref, cw_ref,
                   wout_ref, g2_ref, w1_ref, w2_ref, gf_ref,
                   y_ref, npool_ref, nconv_ref, uhist_ref, zhist_ref):
    t = pl.program_id(1)

    @pl.when(t == 0)
    def _():
        uhist_ref[...] = u0_ref[...]
        zhist_ref[...] = z0_ref[...]

    x = x_ref[...]
    proj = _dot(_rmsnorm(x, g1_ref[...]), win_ref[...])
    u = proj[:, :POOL_W]
    bg = proj[:, POOL_W:POOL_W + CONV_W]
    z = proj[:, POOL_W + CONV_W:POOL_W + 2 * CONV_W] * proj[:, POOL_W + 2 * CONV_W:]

    s = jnp.concatenate([uhist_ref[...], u], axis=0)
    means = []
    for g, w in enumerate(POOL_WINDOWS):
        s = s + pltpu.roll(s, w // 2, axis=0)
        means.append(s[POOL_HIST:, :POOL_GC] * (1.0 / w))
        s = s[:, POOL_GC:]
    d = jnp.concatenate(means, axis=-1) - u
    ya = _pool_out(d, pw_ref, ps_ref)

    zext = jnp.concatenate([zhist_ref[...], z], axis=0)
    conv = (pltpu.roll(zext, 2, axis=0) * cw_ref[0:1, :]
            + pltpu.roll(zext, 1, axis=0) * cw_ref[1:2, :]
            + zext * cw_ref[2:3, :])
    yb = bg * conv[CONV_HIST:]

    y_ref[...] = _post_mixer(x, ya, yb, wout_ref, g2_ref, w1_ref, w2_ref, gf_ref)

    u_tail = u[TILE_T - POOL_HIST:]
    z_tail = z[TILE_T - CONV_HIST:]
    uhist_ref[...] = u_tail
    zhist_ref[...] = z_tail

    @pl.when(t == pl.num_programs(1) - 1)
    def _():
        npool_ref[...] = pltpu.roll(u_tail, POOL_BUF, axis=0)[:POOL_BUF]
        nconv_ref[...] = pltpu.roll(z_tail, CONV_BUF, axis=0)[:CONV_BUF]


def _sample_kernel(xs_ref, meta_ref, sp_ref, sc_ref, g1_ref, win_ref, pw_ref, ps_ref, cw_ref,
                   wout_ref, g2_ref, w1_ref, w2_ref, gf_ref,
                   y_ref, npool_ref, nconv_ref, u0_ref, z0_ref,
                   hn_ref, proj_ref, x_acc_ref, hn2_ref, ext_ref, zext_ref):
    i = pl.program_id(0)
    n_seq, n_new, _ = xs_ref.shape
    n_rows = n_seq * n_new

    def step_rows(j):
        return slice(j * n_seq, (j + 1) * n_seq)

    @pl.when(i == 0)
    def _():
        for j in range(n_new):
            x_acc_ref[step_rows(j), :] = xs_ref[:, j, :]
        hn_ref[:n_rows, :] = _rmsnorm(x_acc_ref[...], g1_ref[...]).astype(jnp.bfloat16)
        hn_ref[n_rows:, :] = _rmsnorm(meta_ref[...], g1_ref[...]).astype(jnp.bfloat16)

    @pl.when(i < N_IN_CHUNKS)
    def _():
        proj_ref[i] = _dot(hn_ref[...], win_ref[...])

    @pl.when(i == MIX_STEP)
    def _():
        u0_ref[...] = proj_ref[0, n_rows:, :]
        z_meta = proj_ref[2, n_rows:, :] * proj_ref[3, n_rows:, :]
        z0_ref[...] = z_meta[N_META - CONV_HIST:]

        for j in range(POOL_BUF):
            ext_ref[j] = sp_ref[:, j, :]
        for j in range(n_new):
            ext_ref[POOL_BUF + j] = proj_ref[0, step_rows(j), :]
        d_steps = []
        for j in range(n_new):
            means = []
            for g, w in enumerate(POOL_WINDOWS):
                acc = ext_ref[POOL_BUF + j, :, _lanes(g)]
                for k in range(1, w):
                    acc = acc + ext_ref[POOL_BUF + j - k, :, _lanes(g)]
                means.append(acc * (1.0 / w))
            d_steps.append(jnp.concatenate(means, axis=-1) - ext_ref[POOL_BUF + j])
        ya = _pool_out(jnp.concatenate(d_steps, axis=0), pw_ref, ps_ref)
        for j in range(POOL_BUF):
            npool_ref[:, j, :] = ext_ref[n_new + j]

        for j in range(CONV_BUF):
            zext_ref[j] = sc_ref[:, j, :]
        for j in range(n_new):
            zext_ref[CONV_BUF + j] = proj_ref[2, step_rows(j), :] * proj_ref[3, step_rows(j), :]
        conv = [zext_ref[j] * cw_ref[0:1, :] + zext_ref[j + 1] * cw_ref[1:2, :]
                + zext_ref[j + 2] * cw_ref[2:3, :] for j in range(n_new)]
        yb = proj_ref[1, :n_rows, :] * jnp.concatenate(conv, axis=0)
        for j in range(CONV_BUF):
            nconv_ref[:, j, :] = zext_ref[n_new + j]

        x_acc_ref[...] += _dot(jnp.concatenate([ya, yb], axis=-1), wout_ref[...])
        hn2_ref[...] = _rmsnorm(x_acc_ref[...], g2_ref[...]).astype(jnp.bfloat16)

    @pl.when(i > MIX_STEP)
    def _():
        a = jnp.maximum(_dot(hn2_ref[...], w1_ref[...]), 0.0)
        x_acc_ref[...] += _dot(a * a, w2_ref[...])

    @pl.when(i == pl.num_programs(0) - 1)
    def _():
        x_acc_ref[...] = _rmsnorm(x_acc_ref[...], gf_ref[...])
        for j in range(n_new):
            y_ref[:, j, :] = x_acc_ref[step_rows(j), :]


def _resident(shape):
    zeros = (0,) * len(shape)
    return pl.BlockSpec(shape, lambda *_: zeros, pipeline_mode=pl.Buffered(1))


def kernel(x_prompt, x_sample, state_pool, state_conv, meta_tokens, norm1_g, w_in, pool_w,
           pool_scale, conv_w, w_out, norm2_g, w1, w2, final_g):
    f32, bf16 = jnp.float32, jnp.bfloat16
    batch, seq, _ = x_prompt.shape
    n_seq, n_new, _ = x_sample.shape
    n_rows = n_seq * n_new
    assert norm1_g.shape[0] == 1 and seq % TILE_T == 0 and seq >= POOL_HIST
    assert n_seq % SUBLANES == 0 and meta_tokens.shape[0] == N_META

    gf = final_g.reshape(1, D_MODEL)

    def ff_chunk(i):
        return jnp.clip(i - (MIX_STEP + 1), 0, N_FF_CHUNKS - 1)

    y_sample, npool_s, nconv_s, u_meta, z_meta = pl.pallas_call(
        _sample_kernel,
        grid=(N_IN_CHUNKS + 1 + N_FF_CHUNKS,),
        in_specs=[_resident(x_sample.shape), _resident(meta_tokens.shape),
                  _resident((None,) + state_pool.shape[1:]),
                  _resident((None,) + state_conv.shape[1:]),
                  _resident(norm1_g.shape),
                  pl.BlockSpec((None, D_MODEL, IN_CHUNK),
                               lambda i: (0, 0, jnp.minimum(i, N_IN_CHUNKS - 1))),
                  _resident((None,) + pool_w.shape[1:]),
                  _resident(pool_scale.shape),
                  _resident((None,) + conv_w.shape[1:]),
                  _resident((None,) + w_out.shape[1:]),
                  _resident(norm2_g.shape),
                  pl.BlockSpec((None, D_MODEL, S_FF_CHUNK), lambda i: (0, 0, ff_chunk(i))),
                  pl.BlockSpec((None, S_FF_CHUNK, D_MODEL), lambda i: (0, ff_chunk(i), 0)),
                  _resident(gf.shape)],
        out_specs=(pl.BlockSpec(x_sample.shape, lambda i: (0, 0, 0)),
                   pl.BlockSpec((None,) + state_pool.shape[1:], lambda i: (0, 0, 0, 0)),
                   pl.BlockSpec((None,) + state_conv.shape[1:], lambda i: (0, 0, 0, 0)),
                   pl.BlockSpec((POOL_HIST, POOL_W), lambda i: (0, 0)),
                   pl.BlockSpec((CONV_HIST, CONV_W), lambda i: (0, 0))),
        out_shape=(jax.ShapeDtypeStruct(x_sample.shape, f32),
                   jax.ShapeDtypeStruct(state_pool.shape, f32),
                   jax.ShapeDtypeStruct(state_conv.shape, f32),
                   jax.ShapeDtypeStruct((POOL_HIST, POOL_W), f32),
                   jax.ShapeDtypeStruct((CONV_HIST, CONV_W), f32)),
        scratch_shapes=[pltpu.VMEM((n_rows + N_META, D_MODEL), bf16),
                        pltpu.VMEM((N_IN_CHUNKS, n_rows + N_META, IN_CHUNK), f32),
                        pltpu.VMEM((n_rows, D_MODEL), f32),
                        pltpu.VMEM((n_rows, D_MODEL), bf16),
                        pltpu.VMEM((POOL_BUF + n_new, n_seq, POOL_W), f32),
                        pltpu.VMEM((CONV_BUF + n_new, n_seq, CONV_W), f32)],
        compiler_params=pltpu.CompilerParams(
            dimension_semantics=("arbitrary",), vmem_limit_bytes=VMEM_LIMIT_BYTES),
        name="sample_layer",
    )(x_sample, meta_tokens, state_pool, state_conv, norm1_g, w_in, pool_w, pool_scale,
      conv_w, w_out, norm2_g, w1, w2, gf)

    weights = (norm1_g, w_in.astype(bf16), pool_w.astype(bf16), pool_scale, conv_w,
               w_out.astype(bf16), norm2_g, w1.astype(bf16), w2.astype(bf16), gf)
    weight_specs = [_resident(w.shape if w.ndim == 2 else (None,) + w.shape[1:])
                    for w in weights]

    n_t = seq // TILE_T
    y_prompt, npool_p, nconv_p = pl.pallas_call(
        _prompt_kernel,
        grid=(batch, n_t),
        in_specs=[pl.BlockSpec((None, TILE_T, D_MODEL), lambda b, t: (b, t, 0)),
                  _resident((POOL_HIST, POOL_W)),
                  _resident((CONV_HIST, CONV_W))] + weight_specs,
        out_specs=(pl.BlockSpec((None, TILE_T, D_MODEL), lambda b, t: (b, t, 0)),
                   pl.BlockSpec((None, None, POOL_BUF, POOL_W), lambda b, t: (0, b, 0, 0)),
                   pl.BlockSpec((None, None, CONV_BUF, CONV_W), lambda b, t: (0, b, 0, 0))),
        out_shape=(jax.ShapeDtypeStruct((batch, seq, D_MODEL), f32),
                   jax.ShapeDtypeStruct((1, batch, POOL_BUF, POOL_W), f32),
                   jax.ShapeDtypeStruct((1, batch, CONV_BUF, CONV_W), f32)),
        scratch_shapes=[pltpu.VMEM((POOL_HIST, POOL_W), f32),
                        pltpu.VMEM((CONV_HIST, CONV_W), f32)],
        compiler_params=pltpu.CompilerParams(
            dimension_semantics=("arbitrary", "arbitrary"),
            vmem_limit_bytes=VMEM_LIMIT_BYTES),
        name="prompt_layer",
    )(x_prompt, u_meta, z_meta, *weights)

    return (y_prompt, y_sample, npool_p, nconv_p, npool_s, nconv_s)
```

```python
import jax
import jax.numpy as jnp
from jax import lax
from jax.experimental import pallas as pl
from jax.experimental.pallas import tpu as pltpu

D_MODEL = 1024
N_META = 16
POOL_WINDOWS = (2, 4, 8, 16)
N_POOL_GROUPS = len(POOL_WINDOWS)
POOL_W = D_MODEL // 2
POOL_GC = POOL_W // N_POOL_GROUPS
POOL_BUF = max(POOL_WINDOWS) - 1
CONV_W = D_MODEL - POOL_W
CONV_K = 3
CONV_BUF = CONV_K - 1
IN_W = POOL_W + 3 * CONV_W
D_FF = 4 * D_MODEL
EPS = 1e-6
PAST_LEN = 16384

SUBLANES = 8
LANES = 128
POOL_HIST = 16
CONV_HIST = 8
TILE_T = 512
FF_CHUNK = 1024
IN_CHUNK = 512
N_IN_CHUNKS = IN_W // IN_CHUNK
S_FF_CHUNK = 512
N_FF_CHUNKS = D_FF // S_FF_CHUNK
MIX_STEP = N_IN_CHUNKS
VMEM_LIMIT_BYTES = 56 * 1024 * 1024

assert POOL_GC == LANES and POOL_BUF <= POOL_HIST and CONV_BUF <= CONV_HIST
assert IN_CHUNK == POOL_W == CONV_W
assert N_META + 1 >= max(POOL_WINDOWS) and PAST_LEN + 1 >= max(POOL_WINDOWS)


def _rmsnorm(x, g):
    y = x * lax.rsqrt(jnp.mean(x * x, axis=-1, keepdims=True) + EPS)
    return y * g


def _dot(a, b):
    return jnp.dot(a.astype(jnp.bfloat16), b.astype(jnp.bfloat16),
                   preferred_element_type=jnp.float32)


def _lanes(g):
    return slice(g * LANES, (g + 1) * LANES)


def _pool_out(d, pw_ref, ps_ref):
    outs = [_dot(d[:, _lanes(g)], pw_ref[g]) for g in range(N_POOL_GROUPS)]
    return jnp.concatenate(outs, axis=-1) * ps_ref[...]


def _post_mixer(x, ya, yb, wout_ref, g2_ref, w1_ref, w2_ref, gf_ref):
    mix = jnp.concatenate([ya, yb], axis=-1)
    x = x + _dot(mix, wout_ref[...])
    hn = _rmsnorm(x, g2_ref[...]).astype(jnp.bfloat16)
    acc = None
    for c in range(D_FF // FF_CHUNK):
        cols = slice(c * FF_CHUNK, (c + 1) * FF_CHUNK)
        a = jnp.maximum(_dot(hn, w1_ref[:, cols]), 0.0)
        part = _dot(a * a, w2_ref[cols, :])
        acc = part if acc is None else acc + part
    x = x + acc
    return _rmsnorm(x, gf_ref[...])


def _prompt_kernel(x_ref, u0_ref, z0_ref, g1_ref, win_ref, pw_ref, ps_ref, cw_ref,
                   wout_ref, g2_ref, w1_ref, w2_ref, gf_ref,
                   y_ref, npool_ref, nconv_ref, uhist_ref, zhist_ref, utail_ref):
    t = pl.program_id(1)

    @pl.when(t == 0)
    def _():
        uhist_ref[...] = u0_ref[...]
        zhist_ref[...] = z0_ref[...]

    x = x_ref[...]
    proj = _dot(_rmsnorm(x, g1_ref[...]), win_ref[...])
    u = proj[:, :POOL_W]
    bg = proj[:, POOL_W:POOL_W + CONV_W]
    z = proj[:, POOL_W + CONV_W:POOL_W + 2 * CONV_W] * proj[:, POOL_W + 2 * CONV_W:]

    s = jnp.concatenate([uhist_ref[...], u], axis=0)
    means = []
    for g, w in enumerate(POOL_WINDOWS):
        s = s + pltpu.roll(s, w // 2, axis=0)
        means.append(s[POOL_HIST:, :POOL_GC] * (1.0 / w))
        s = s[:, POOL_GC:]
    d = jnp.concatenate(means, axis=-1) - u
    ya = _pool_out(d, pw_ref, ps_ref)

    zext = jnp.concatenate([zhist_ref[...], z], axis=0)
    conv = (pltpu.roll(zext, 2, axis=0) * cw_ref[0]
            + pltpu.roll(zext, 1, axis=0) * cw_ref[1]
            + zext * cw_ref[2])
    yb = bg * conv[CONV_HIST:]

    y_ref[...] = _post_mixer(x, ya, yb, wout_ref, g2_ref, w1_ref, w2_ref, gf_ref)

    u_tail = u[TILE_T - POOL_HIST:]
    z_tail = z[TILE_T - CONV_HIST:]
    uhist_ref[...] = u_tail
    zhist_ref[...] = z_tail

    b = pl.program_id(0)
    last_tile = t == pl.num_programs(1) - 1

    @pl.when(last_tile)
    def _():
        utail_ref[b] = u_tail
        nconv_ref[...] = pltpu.roll(z_tail, CONV_BUF, axis=0)[:CONV_BUF]

    @pl.when(jnp.logical_and(last_tile, b == pl.num_programs(0) - 1))
    def _():
        for bb in range(npool_ref.shape[1]):
            npool_ref[:, bb, :] = pltpu.roll(utail_ref[bb], POOL_BUF, axis=0)[:POOL_BUF]


def _sample_kernel(xs_ref, meta_ref, sp_ref, sc_ref, g1_ref, win_ref, pw_ref, ps_ref, cw_ref,
                   wout_ref, g2_ref, w1_ref, w2_ref, gf_ref,
                   y_ref, npool_ref, nconv_ref, u0_ref, z0_ref,
                   hn_ref, proj_ref, x_acc_ref, hn2_ref, zext_ref):
    i = pl.program_id(0)
    n_seq, n_new, _ = xs_ref.shape
    n_rows = n_seq * n_new

    def step_rows(j):
        return slice(j * n_seq, (j + 1) * n_seq)

    @pl.when(i == 0)
    def _():
        for j in range(n_new):
            x_acc_ref[step_rows(j), :] = xs_ref[:, j, :]
        hn_ref[:n_rows, :] = _rmsnorm(x_acc_ref[...], g1_ref[...]).astype(jnp.bfloat16)
        hn_ref[n_rows:, :] = _rmsnorm(meta_ref[...], g1_ref[...]).astype(jnp.bfloat16)

    @pl.when(i < N_IN_CHUNKS)
    def _():
        proj_ref[i] = _dot(hn_ref[...], win_ref[...])

    @pl.when(i == MIX_STEP)
    def _():
        u0_ref[...] = proj_ref[0, n_rows:, :]
        z_meta = proj_ref[2, n_rows:, :] * proj_ref[3, n_rows:, :]
        z0_ref[...] = z_meta[N_META - CONV_HIST:]

        def ext(j, lanes=slice(None)):
            if j < POOL_BUF:
                return sp_ref[j, :, lanes]
            return proj_ref[0, step_rows(j - POOL_BUF), lanes]

        d_steps = []
        for j in range(n_new):
            means = []
            for g, w in enumerate(POOL_WINDOWS):
                acc = ext(POOL_BUF + j, _lanes(g))
                for k in range(1, w):
                    acc = acc + ext(POOL_BUF + j - k, _lanes(g))
                means.append(acc * (1.0 / w))
            d_steps.append(jnp.concatenate(means, axis=-1) - ext(POOL_BUF + j))
        ya = _pool_out(jnp.concatenate(d_steps, axis=0), pw_ref, ps_ref)
        for j in range(POOL_BUF):
            npool_ref[j] = ext(n_new + j)

        for j in range(CONV_BUF):
            zext_ref[j] = sc_ref[:, j, :]
        for j in range(n_new):
            zext_ref[CONV_BUF + j] = proj_ref[2, step_rows(j), :] * proj_ref[3, step_rows(j), :]
        conv = [zext_ref[j] * cw_ref[0] + zext_ref[j + 1] * cw_ref[1]
                + zext_ref[j + 2] * cw_ref[2] for j in range(n_new)]
        yb = proj_ref[1, :n_rows, :] * jnp.concatenate(conv, axis=0)
        for j in range(CONV_BUF):
            nconv_ref[:, j, :] = zext_ref[n_new + j]

        x_acc_ref[...] += _dot(jnp.concatenate([ya, yb], axis=-1), wout_ref[...])
        hn2_ref[...] = _rmsnorm(x_acc_ref[...], g2_ref[...]).astype(jnp.bfloat16)

    @pl.when(i > MIX_STEP)
    def _():
        a = jnp.maximum(_dot(hn2_ref[...], w1_ref[...]), 0.0)
        x_acc_ref[...] += _dot(a * a, w2_ref[...])

    @pl.when(i == pl.num_programs(0) - 1)
    def _():
        x_acc_ref[...] = _rmsnorm(x_acc_ref[...], gf_ref[...])
        for j in range(n_new):
            y_ref[:, j, :] = x_acc_ref[step_rows(j), :]


def _resident(shape):
    zeros = (0,) * len(shape)
    return pl.BlockSpec(shape, lambda *_: zeros, pipeline_mode=pl.Buffered(1))


def kernel(x_prompt, x_sample, state_pool, state_conv, meta_tokens, norm1_g, w_in, pool_w,
           pool_scale, conv_w, w_out, norm2_g, w1, w2, final_g):
    f32, bf16 = jnp.float32, jnp.bfloat16
    batch, seq, _ = x_prompt.shape
    n_seq, n_new, _ = x_sample.shape
    n_rows = n_seq * n_new
    assert norm1_g.shape[0] == 1 and seq % TILE_T == 0 and seq >= POOL_HIST
    assert n_seq % SUBLANES == 0 and meta_tokens.shape[0] == N_META

    gf = final_g.reshape(1, D_MODEL)
    sp_steps = jnp.transpose(state_pool, (0, 2, 1, 3))
    cw = jnp.transpose(conv_w, (1, 0, 2))

    def ff_chunk(i):
        return jnp.clip(i - (MIX_STEP + 1), 0, N_FF_CHUNKS - 1)

    y_sample, npool_s, nconv_s, u_meta, z_meta = pl.pallas_call(
        _sample_kernel,
        grid=(N_IN_CHUNKS + 1 + N_FF_CHUNKS,),
        in_specs=[_resident(x_sample.shape), _resident(meta_tokens.shape),
                  _resident((None,) + sp_steps.shape[1:]),
                  _resident((None,) + state_conv.shape[1:]),
                  _resident(norm1_g.shape),
                  pl.BlockSpec((None, D_MODEL, IN_CHUNK),
                               lambda i: (0, 0, jnp.minimum(i, N_IN_CHUNKS - 1))),
                  _resident((None,) + pool_w.shape[1:]),
                  _resident(pool_scale.shape),
                  _resident(cw.shape),
                  _resident((None,) + w_out.shape[1:]),
                  _resident(norm2_g.shape),
                  pl.BlockSpec((None, D_MODEL, S_FF_CHUNK), lambda i: (0, 0, ff_chunk(i))),
                  pl.BlockSpec((None, S_FF_CHUNK, D_MODEL), lambda i: (0, ff_chunk(i), 0)),
                  _resident(gf.shape)],
        out_specs=(pl.BlockSpec(x_sample.shape, lambda i: (0, 0, 0)),
                   pl.BlockSpec((None,) + sp_steps.shape[1:], lambda i: (0, 0, 0, 0)),
                   pl.BlockSpec((None,) + state_conv.shape[1:], lambda i: (0, 0, 0, 0)),
                   pl.BlockSpec((POOL_HIST, POOL_W), lambda i: (0, 0)),
                   pl.BlockSpec((CONV_HIST, CONV_W), lambda i: (0, 0))),
        out_shape=(jax.ShapeDtypeStruct(x_sample.shape, f32),
                   jax.ShapeDtypeStruct(sp_steps.shape, f32),
                   jax.ShapeDtypeStruct(state_conv.shape, f32),
                   jax.ShapeDtypeStruct((POOL_HIST, POOL_W), f32),
                   jax.ShapeDtypeStruct((CONV_HIST, CONV_W), f32)),
        scratch_shapes=[pltpu.VMEM((n_rows + N_META, D_MODEL), bf16),
                        pltpu.VMEM((N_IN_CHUNKS, n_rows + N_META, IN_CHUNK), f32),
                        pltpu.VMEM((n_rows, D_MODEL), f32),
                        pltpu.VMEM((n_rows, D_MODEL), bf16),
                        pltpu.VMEM((CONV_BUF + n_new, n_seq, CONV_W), f32)],
        compiler_params=pltpu.CompilerParams(
            dimension_semantics=("arbitrary",), vmem_limit_bytes=VMEM_LIMIT_BYTES),
        name="sample_layer",
    )(x_sample, meta_tokens, sp_steps, state_conv, norm1_g, w_in, pool_w, pool_scale,
      cw, w_out, norm2_g, w1, w2, gf)

    weights = (norm1_g, w_in.astype(bf16), pool_w.astype(bf16), pool_scale, cw,
               w_out.astype(bf16), norm2_g, w1.astype(bf16), w2.astype(bf16), gf)
    weight_specs = [_resident((None,) + w.shape[1:] if w.ndim > 2 and w.shape[0] == 1
                              else w.shape) for w in weights]

    n_t = seq // TILE_T
    y_prompt, npool_p, nconv_p = pl.pallas_call(
        _prompt_kernel,
        grid=(batch, n_t),
        in_specs=[pl.BlockSpec((None, TILE_T, D_MODEL), lambda b, t: (b, t, 0)),
                  _resident((POOL_HIST, POOL_W)),
                  _resident((CONV_HIST, CONV_W))] + weight_specs,
        out_specs=(pl.BlockSpec((None, TILE_T, D_MODEL), lambda b, t: (b, t, 0)),
                   pl.BlockSpec((None, POOL_BUF, batch, POOL_W), lambda b, t: (0, 0, 0, 0)),
                   pl.BlockSpec((None, None, CONV_BUF, CONV_W), lambda b, t: (0, b, 0, 0))),
        out_shape=(jax.ShapeDtypeStruct((batch, seq, D_MODEL), f32),
                   jax.ShapeDtypeStruct((1, POOL_BUF, batch, POOL_W), f32),
                   jax.ShapeDtypeStruct((1, batch, CONV_BUF, CONV_W), f32)),
        scratch_shapes=[pltpu.VMEM((POOL_HIST, POOL_W), f32),
                        pltpu.VMEM((CONV_HIST, CONV_W), f32),
                        pltpu.VMEM((batch, POOL_HIST, POOL_W), f32)],
        compiler_params=pltpu.CompilerParams(
            dimension_semantics=("arbitrary", "arbitrary"),
            vmem_limit_bytes=VMEM_LIMIT_BYTES),
        name="prompt_layer",
    )(x_prompt, u_meta, z_meta, *weights)

    return (y_prompt, y_sample,
            jnp.transpose(npool_p, (0, 2, 1, 3)), nconv_p,
            jnp.transpose(npool_s, (0, 2, 1, 3)), nconv_s)
```

```python
import jax
import jax.numpy as jnp
from jax import lax
from jax.experimental import pallas as pl
from jax.experimental.pallas import tpu as pltpu

D_MODEL = 1024
N_META = 16
POOL_WINDOWS = (2, 4, 8, 16)
N_POOL_GROUPS = len(POOL_WINDOWS)
POOL_W = D_MODEL // 2
POOL_GC = POOL_W // N_POOL_GROUPS
POOL_BUF = max(POOL_WINDOWS) - 1
CONV_W = D_MODEL - POOL_W
CONV_K = 3
CONV_BUF = CONV_K - 1
IN_W = POOL_W + 3 * CONV_W
D_FF = 4 * D_MODEL
EPS = 1e-6
PAST_LEN = 16384

SUBLANES = 8
LANES = 128
POOL_HIST = 16
CONV_HIST = 8
TILE_T = 512
FF_CHUNK = 1024
IN_CHUNK = 512
N_IN_CHUNKS = IN_W // IN_CHUNK
S_FF_CHUNK = 512
N_FF_CHUNKS = D_FF // S_FF_CHUNK
MIX_STEP = N_IN_CHUNKS
VMEM_LIMIT_BYTES = 56 * 1024 * 1024

assert POOL_GC == LANES and POOL_BUF <= POOL_HIST and CONV_BUF <= CONV_HIST
assert IN_CHUNK == POOL_W == CONV_W
assert N_META + 1 >= max(POOL_WINDOWS) and PAST_LEN + 1 >= max(POOL_WINDOWS)


def _rmsnorm(x, g):
    y = x * lax.rsqrt(jnp.mean(x * x, axis=-1, keepdims=True) + EPS)
    return y * g


def _dot(a, b):
    return jnp.dot(a.astype(jnp.bfloat16), b.astype(jnp.bfloat16),
                   preferred_element_type=jnp.float32)


def _lanes(g):
    return slice(g * LANES, (g + 1) * LANES)


def _pool_out(d, pw_ref, ps_ref):
    outs = [_dot(d[:, _lanes(g)], pw_ref[g]) for g in range(N_POOL_GROUPS)]
    return jnp.concatenate(outs, axis=-1) * ps_ref[...]


def _post_mixer(x, ya, yb, wout_ref, g2_ref, w1_ref, w2_ref, gf_ref):
    mix = jnp.concatenate([ya, yb], axis=-1)
    x = x + _dot(mix, wout_ref[...])
    hn = _rmsnorm(x, g2_ref[...]).astype(jnp.bfloat16)
    acc = None
    for c in range(D_FF // FF_CHUNK):
        cols = slice(c * FF_CHUNK, (c + 1) * FF_CHUNK)
        a = jnp.maximum(_dot(hn, w1_ref[:, cols]), 0.0)
        part = _dot(a * a, w2_ref[cols, :])
        acc = part if acc is None else acc + part
    x = x + acc
    return _rmsnorm(x, gf_ref[...])


def _prompt_kernel(x_ref, u0_ref, z0_ref, g1_ref, win_ref, pw_ref, ps_ref, cw_ref,
                   wout_ref, g2_ref, w1_ref, w2_ref, gf_ref,
                   y_ref, npool_ref, nconv_ref, uhist_ref, zhist_ref, utail_ref):
    t = pl.program_id(1)

    @pl.when(t == 0)
    def _():
        uhist_ref[...] = u0_ref[...]
        zhist_ref[...] = z0_ref[...]

    x = x_ref[...]
    proj = _dot(_rmsnorm(x, g1_ref[...]), win_ref[...])
    u = proj[:, :POOL_W]
    bg = proj[:, POOL_W:POOL_W + CONV_W]
    z = proj[:, POOL_W + CONV_W:POOL_W + 2 * CONV_W] * proj[:, POOL_W + 2 * CONV_W:]

    s = jnp.concatenate([uhist_ref[...], u], axis=0)
    means = []
    for g, w in enumerate(POOL_WINDOWS):
        s = s + pltpu.roll(s, w // 2, axis=0)
        means.append(s[POOL_HIST:, :POOL_GC] * (1.0 / w))
        s = s[:, POOL_GC:]
    d = jnp.concatenate(means, axis=-1) - u
    ya = _pool_out(d, pw_ref, ps_ref)

    zext = jnp.concatenate([zhist_ref[...], z], axis=0)
    conv = (pltpu.roll(zext, 2, axis=0) * cw_ref[0]
            + pltpu.roll(zext, 1, axis=0) * cw_ref[1]
            + zext * cw_ref[2])
    yb = bg * conv[CONV_HIST:]

    y_ref[...] = _post_mixer(x, ya, yb, wout_ref, g2_ref, w1_ref, w2_ref, gf_ref)

    u_tail = u[TILE_T - POOL_HIST:]
    z_tail = z[TILE_T - CONV_HIST:]
    uhist_ref[...] = u_tail
    zhist_ref[...] = z_tail

    b = pl.program_id(0)
    last_tile = t == pl.num_programs(1) - 1

    @pl.when(last_tile)
    def _():
        utail_ref[b] = u_tail
        nconv_ref[...] = pltpu.roll(z_tail, CONV_BUF, axis=0)[:CONV_BUF]

    @pl.when(jnp.logical_and(last_tile, b == pl.num_programs(0) - 1))
    def _():
        for bb in range(npool_ref.shape[1]):
            npool_ref[:, bb, :] = pltpu.roll(utail_ref[bb], POOL_BUF, axis=0)[:POOL_BUF]


def _sample_kernel(xs_ref, meta_ref, sp_ref, sc_ref, g1_ref, win_ref, pw_ref, ps_ref, cw_ref,
                   wout_ref, g2_ref, w1_ref, w2_ref, gf_ref,
                   y_ref, npool_ref, nconv_ref, u0_ref, z0_ref,
                   win_bf_ref, pw_bf_ref, wout_bf_ref, w1_bf_ref, w2_bf_ref,
                   hn_ref, proj_ref, x_acc_ref, hn2_ref, zext_ref):
    i = pl.program_id(0)
    n_seq, n_new, _ = xs_ref.shape
    n_rows = n_seq * n_new

    def step_rows(j):
        return slice(j * n_seq, (j + 1) * n_seq)

    @pl.when(i == 0)
    def _():
        for j in range(n_new):
            x_acc_ref[step_rows(j), :] = xs_ref[:, j, :]
        hn_ref[:n_rows, :] = _rmsnorm(x_acc_ref[...], g1_ref[...]).astype(jnp.bfloat16)
        hn_ref[n_rows:, :] = _rmsnorm(meta_ref[...], g1_ref[...]).astype(jnp.bfloat16)

    @pl.when(i < N_IN_CHUNKS)
    def _():
        win_bf_ref[...] = win_ref[...].astype(jnp.bfloat16)
        proj_ref[i] = _dot(hn_ref[...], win_bf_ref[...])

    @pl.when(i == MIX_STEP)
    def _():
        u0_ref[...] = proj_ref[0, n_rows:, :]
        z_meta = proj_ref[2, n_rows:, :] * proj_ref[3, n_rows:, :]
        z0_ref[...] = z_meta[N_META - CONV_HIST:]

        def ext(j, lanes=slice(None)):
            if j < POOL_BUF:
                return sp_ref[j, :, lanes]
            return proj_ref[0, step_rows(j - POOL_BUF), lanes]

        d_steps = []
        for j in range(n_new):
            means = []
            for g, w in enumerate(POOL_WINDOWS):
                acc = ext(POOL_BUF + j, _lanes(g))
                for k in range(1, w):
                    acc = acc + ext(POOL_BUF + j - k, _lanes(g))
                means.append(acc * (1.0 / w))
            d_steps.append(jnp.concatenate(means, axis=-1) - ext(POOL_BUF + j))
        pw_bf_ref[...] = pw_ref[...].astype(jnp.bfloat16)
        ya = _pool_out(jnp.concatenate(d_steps, axis=0), pw_bf_ref, ps_ref)
        for j in range(POOL_BUF):
            npool_ref[j] = ext(n_new + j)

        for j in range(CONV_BUF):
            zext_ref[j] = sc_ref[:, j, :]
        for j in range(n_new):
            zext_ref[CONV_BUF + j] = proj_ref[2, step_rows(j), :] * proj_ref[3, step_rows(j), :]
        conv = [zext_ref[j] * cw_ref[0] + zext_ref[j + 1] * cw_ref[1]
                + zext_ref[j + 2] * cw_ref[2] for j in range(n_new)]
        yb = proj_ref[1, :n_rows, :] * jnp.concatenate(conv, axis=0)
        for j in range(CONV_BUF):
            nconv_ref[:, j, :] = zext_ref[n_new + j]

        wout_bf_ref[...] = wout_ref[...].astype(jnp.bfloat16)
        x_acc_ref[...] += _dot(jnp.concatenate([ya, yb], axis=-1), wout_bf_ref[...])
        hn2_ref[...] = _rmsnorm(x_acc_ref[...], g2_ref[...]).astype(jnp.bfloat16)

    @pl.when(i > MIX_STEP)
    def _():
        w1_bf_ref[...] = w1_ref[...].astype(jnp.bfloat16)
        w2_bf_ref[...] = w2_ref[...].astype(jnp.bfloat16)
        a = jnp.maximum(_dot(hn2_ref[...], w1_bf_ref[...]), 0.0)
        x_acc_ref[...] += _dot(a * a, w2_bf_ref[...])

    @pl.when(i == pl.num_programs(0) - 1)
    def _():
        x_acc_ref[...] = _rmsnorm(x_acc_ref[...], gf_ref[...])
        for j in range(n_new):
            y_ref[:, j, :] = x_acc_ref[step_rows(j), :]


def _resident(shape):
    zeros = (0,) * len(shape)
    return pl.BlockSpec(shape, lambda *_: zeros, pipeline_mode=pl.Buffered(1))


def kernel(x_prompt, x_sample, state_pool, state_conv, meta_tokens, norm1_g, w_in, pool_w,
           pool_scale, conv_w, w_out, norm2_g, w1, w2, final_g):
    f32, bf16 = jnp.float32, jnp.bfloat16
    batch, seq, _ = x_prompt.shape
    n_seq, n_new, _ = x_sample.shape
    n_rows = n_seq * n_new
    assert norm1_g.shape[0] == 1 and seq % TILE_T == 0 and seq >= POOL_HIST
    assert n_seq % SUBLANES == 0 and meta_tokens.shape[0] == N_META

    gf = final_g.reshape(1, D_MODEL)
    sp_steps = jnp.transpose(state_pool, (0, 2, 1, 3))
    cw = jnp.transpose(conv_w, (1, 0, 2))

    def ff_chunk(i):
        return jnp.clip(i - (MIX_STEP + 1), 0, N_FF_CHUNKS - 1)

    in_chunk_spec = pl.BlockSpec((None, D_MODEL, IN_CHUNK),
                                 lambda i: (0, 0, jnp.minimum(i, N_IN_CHUNKS - 1)))
    w1_chunk_spec = pl.BlockSpec((None, D_MODEL, S_FF_CHUNK), lambda i: (0, 0, ff_chunk(i)))
    w2_chunk_spec = pl.BlockSpec((None, S_FF_CHUNK, D_MODEL), lambda i: (0, ff_chunk(i), 0))

    (y_sample, npool_s, nconv_s, u_meta, z_meta,
     win_bf, pw_bf, wout_bf, w1_bf, w2_bf) = pl.pallas_call(
        _sample_kernel,
        grid=(N_IN_CHUNKS + 1 + N_FF_CHUNKS,),
        in_specs=[_resident(x_sample.shape), _resident(meta_tokens.shape),
                  _resident((None,) + sp_steps.shape[1:]),
                  _resident((None,) + state_conv.shape[1:]),
                  _resident(norm1_g.shape),
                  in_chunk_spec,
                  _resident((None,) + pool_w.shape[1:]),
                  _resident(pool_scale.shape),
                  _resident(cw.shape),
                  _resident((None,) + w_out.shape[1:]),
                  _resident(norm2_g.shape),
                  w1_chunk_spec, w2_chunk_spec,
                  _resident(gf.shape)],
        out_specs=(pl.BlockSpec(x_sample.shape, lambda i: (0, 0, 0)),
                   pl.BlockSpec((None,) + sp_steps.shape[1:], lambda i: (0, 0, 0, 0)),
                   pl.BlockSpec((None,) + state_conv.shape[1:], lambda i: (0, 0, 0, 0)),
                   pl.BlockSpec((POOL_HIST, POOL_W), lambda i: (0, 0)),
                   pl.BlockSpec((CONV_HIST, CONV_W), lambda i: (0, 0)),
                   in_chunk_spec,
                   pl.BlockSpec((None,) + pool_w.shape[1:], lambda i: (0, 0, 0, 0)),
                   pl.BlockSpec((None,) + w_out.shape[1:], lambda i: (0, 0, 0)),
                   w1_chunk_spec, w2_chunk_spec),
        out_shape=(jax.ShapeDtypeStruct(x_sample.shape, f32),
                   jax.ShapeDtypeStruct(sp_steps.shape, f32),
                   jax.ShapeDtypeStruct(state_conv.shape, f32),
                   jax.ShapeDtypeStruct((POOL_HIST, POOL_W), f32),
                   jax.ShapeDtypeStruct((CONV_HIST, CONV_W), f32),
                   jax.ShapeDtypeStruct(w_in.shape, bf16),
                   jax.ShapeDtypeStruct(pool_w.shape, bf16),
                   jax.ShapeDtypeStruct(w_out.shape, bf16),
                   jax.ShapeDtypeStruct(w1.shape, bf16),
                   jax.ShapeDtypeStruct(w2.shape, bf16)),
        scratch_shapes=[pltpu.VMEM((n_rows + N_META, D_MODEL), bf16),
                        pltpu.VMEM((N_IN_CHUNKS, n_rows + N_META, IN_CHUNK), f32),
                        pltpu.VMEM((n_rows, D_MODEL), f32),
                        pltpu.VMEM((n_rows, D_MODEL), bf16),
                        pltpu.VMEM((CONV_BUF + n_new, n_seq, CONV_W), f32)],
        compiler_params=pltpu.CompilerParams(
            dimension_semantics=("arbitrary",), vmem_limit_bytes=VMEM_LIMIT_BYTES),
        name="sample_layer",
    )(x_sample, meta_tokens, sp_steps, state_conv, norm1_g, w_in, pool_w, pool_scale,
      cw, w_out, norm2_g, w1, w2, gf)

    weights = (norm1_g, win_bf, pw_bf, pool_scale, cw, wout_bf, norm2_g, w1_bf, w2_bf, gf)
    weight_specs = [_resident((None,) + w.shape[1:] if w.ndim > 2 and w.shape[0] == 1
                              else w.shape) for w in weights]

    n_t = seq // TILE_T
    y_prompt, npool_p, nconv_p = pl.pallas_call(
        _prompt_kernel,
        grid=(batch, n_t),
        in_specs=[pl.BlockSpec((None, TILE_T, D_MODEL), lambda b, t: (b, t, 0)),
                  _resident((POOL_HIST, POOL_W)),
                  _resident((CONV_HIST, CONV_W))] + weight_specs,
        out_specs=(pl.BlockSpec((None, TILE_T, D_MODEL), lambda b, t: (b, t, 0)),
                   pl.BlockSpec((None, POOL_BUF, batch, POOL_W), lambda b, t: (0, 0, 0, 0)),
                   pl.BlockSpec((None, None, CONV_BUF, CONV_W), lambda b, t: (0, b, 0, 0))),
        out_shape=(jax.ShapeDtypeStruct((batch, seq, D_MODEL), f32),
                   jax.ShapeDtypeStruct((1, POOL_BUF, batch, POOL_W), f32),
                   jax.ShapeDtypeStruct((1, batch, CONV_BUF, CONV_W), f32)),
        scratch_shapes=[pltpu.VMEM((POOL_HIST, POOL_W), f32),
                        pltpu.VMEM((CONV_HIST, CONV_W), f32),
                        pltpu.VMEM((batch, POOL_HIST, POOL_W), f32)],
        compiler_params=pltpu.CompilerParams(
            dimension_semantics=("arbitrary", "arbitrary"),
            vmem_limit_bytes=VMEM_LIMIT_BYTES),
        name="prompt_layer",
    )(x_prompt, u_meta, z_meta, *weights)

    return (y_prompt, y_sample,
            jnp.transpose(npool_p, (0, 2, 1, 3)), nconv_p,
            jnp.transpose(npool_s, (0, 2, 1, 3)), nconv_s)
```

```python
import jax
import jax.numpy as jnp
from jax import lax
from jax.experimental import pallas as pl
from jax.experimental.pallas import tpu as pltpu

D_MODEL = 1024
N_META = 16
POOL_WINDOWS = (2, 4, 8, 16)
N_POOL_GROUPS = len(POOL_WINDOWS)
POOL_W = D_MODEL // 2
POOL_GC = POOL_W // N_POOL_GROUPS
POOL_BUF = max(POOL_WINDOWS) - 1
CONV_W = D_MODEL - POOL_W
CONV_K = 3
CONV_BUF = CONV_K - 1
IN_W = POOL_W + 3 * CONV_W
D_FF = 4 * D_MODEL
EPS = 1e-6
PAST_LEN = 16384

SUBLANES = 8
LANES = 128
POOL_HIST = 16
CONV_HIST = 8
TILE_T = 1024
SUB_T = 512
FF_CHUNK = 1024
IN_CHUNK = 512
N_IN_CHUNKS = IN_W // IN_CHUNK
S_FF_CHUNK = 512
N_FF_CHUNKS = D_FF // S_FF_CHUNK
MIX_STEP = N_IN_CHUNKS
VMEM_LIMIT_BYTES = 56 * 1024 * 1024

assert POOL_GC == LANES and POOL_BUF <= POOL_HIST and CONV_BUF <= CONV_HIST
assert IN_CHUNK == POOL_W == CONV_W
assert N_META + 1 >= max(POOL_WINDOWS) and PAST_LEN + 1 >= max(POOL_WINDOWS)


def _rmsnorm(x, g):
    y = x * lax.rsqrt(jnp.mean(x * x, axis=-1, keepdims=True) + EPS)
    return y * g


def _dot(a, b):
    return jnp.dot(a.astype(jnp.bfloat16), b.astype(jnp.bfloat16),
                   preferred_element_type=jnp.float32)


def _lanes(g):
    return slice(g * LANES, (g + 1) * LANES)


def _pool_out(d, pw_ref, ps_ref):
    outs = [_dot(d[:, _lanes(g)], pw_ref[g]) for g in range(N_POOL_GROUPS)]
    return jnp.concatenate(outs, axis=-1) * ps_ref[...]


def _mixers(proj, u_hist, z_hist, cw_ref):
    n = proj.shape[0]
    u = proj[:, :POOL_W]
    bg = proj[:, POOL_W:POOL_W + CONV_W]
    z = proj[:, POOL_W + CONV_W:POOL_W + 2 * CONV_W] * proj[:, POOL_W + 2 * CONV_W:]

    s = jnp.concatenate([u_hist, u], axis=0)
    means = []
    for g, w in enumerate(POOL_WINDOWS):
        s = s + pltpu.roll(s, w // 2, axis=0)
        means.append(s[POOL_HIST:, :POOL_GC] * (1.0 / w))
        s = s[:, POOL_GC:]
    d = jnp.concatenate(means, axis=-1) - u

    zext = jnp.concatenate([z_hist, z], axis=0)
    conv = (pltpu.roll(zext, 2, axis=0) * cw_ref[0]
            + pltpu.roll(zext, 1, axis=0) * cw_ref[1]
            + zext * cw_ref[2])
    yb = bg * conv[CONV_HIST:]
    return d, yb, u[n - POOL_HIST:], z[n - CONV_HIST:]


def _prompt_kernel(x_ref, u0_ref, z0_ref, g1_ref, win_ref, pw_ref, ps_ref, cw_ref,
                   wout_ref, g2_ref, w1_ref, w2_ref, gf_ref,
                   y_ref, npool_ref, nconv_ref, uhist_ref, zhist_ref, utail_ref):
    t = pl.program_id(1)

    @pl.when(t == 0)
    def _():
        uhist_ref[...] = u0_ref[...]
        zhist_ref[...] = z0_ref[...]

    subs = [slice(k * SUB_T, (k + 1) * SUB_T) for k in range(TILE_T // SUB_T)]
    xs = [x_ref[rows, :] for rows in subs]
    projs = [_dot(_rmsnorm(x, g1_ref[...]), win_ref[...]) for x in xs]

    u_tail, z_tail = uhist_ref[...], zhist_ref[...]
    x1s, hn2s = [], []
    for x, proj in zip(xs, projs):
        d, yb, u_tail, z_tail = _mixers(proj, u_tail, z_tail, cw_ref)
        out = _dot(yb, wout_ref[POOL_W:, :])
        out = out + _dot(_pool_out(d, pw_ref, ps_ref), wout_ref[:POOL_W, :])
        x1 = x + out
        x1s.append(x1)
        hn2s.append(_rmsnorm(x1, g2_ref[...]).astype(jnp.bfloat16))
    uhist_ref[...] = u_tail
    zhist_ref[...] = z_tail

    accs = [None] * len(subs)
    for c in range(D_FF // FF_CHUNK):
        cols = slice(c * FF_CHUNK, (c + 1) * FF_CHUNK)
        acts = [jnp.maximum(_dot(hn2, w1_ref[:, cols]), 0.0) for hn2 in hn2s]
        for k, a in enumerate(acts):
            part = _dot(a * a, w2_ref[cols, :])
            accs[k] = part if accs[k] is None else accs[k] + part
    for rows, x1, acc in zip(subs, x1s, accs):
        y_ref[rows, :] = _rmsnorm(x1 + acc, gf_ref[...])

    b = pl.program_id(0)
    last_tile = t == pl.num_programs(1) - 1

    @pl.when(last_tile)
    def _():
        utail_ref[b] = u_tail
        nconv_ref[...] = pltpu.roll(z_tail, CONV_BUF, axis=0)[:CONV_BUF]

    @pl.when(jnp.logical_and(last_tile, b == pl.num_programs(0) - 1))
    def _():
        for bb in range(npool_ref.shape[1]):
            npool_ref[:, bb, :] = pltpu.roll(utail_ref[bb], POOL_BUF, axis=0)[:POOL_BUF]


def _sample_kernel(xs_ref, meta_ref, sp_ref, sc_ref, g1_ref, win_ref, pw_ref, ps_ref, cw_ref,
                   wout_ref, g2_ref, w1_ref, w2_ref, gf_ref,
                   y_ref, npool_ref, nconv_ref, u0_ref, z0_ref,
                   win_bf_ref, pw_bf_ref, wout_bf_ref, w1_bf_ref, w2_bf_ref,
                   hn_ref, proj_ref, x_acc_ref, hn2_ref, zext_ref):
    i = pl.program_id(0)
    n_seq, n_new, _ = xs_ref.shape
    n_rows = n_seq * n_new

    def step_rows(j):
        return slice(j * n_seq, (j + 1) * n_seq)

    @pl.when(i == 0)
    def _():
        for j in range(n_new):
            x_acc_ref[step_rows(j), :] = xs_ref[:, j, :]
        hn_ref[:n_rows, :] = _rmsnorm(x_acc_ref[...], g1_ref[...]).astype(jnp.bfloat16)
        hn_ref[n_rows:, :] = _rmsnorm(meta_ref[...], g1_ref[...]).astype(jnp.bfloat16)

    @pl.when(i < N_IN_CHUNKS)
    def _():
        win_bf_ref[...] = win_ref[...].astype(jnp.bfloat16)
        proj_ref[i] = _dot(hn_ref[...], win_bf_ref[...])

    @pl.when(i == MIX_STEP)
    def _():
        u0_ref[...] = proj_ref[0, n_rows:, :]
        z_meta = proj_ref[2, n_rows:, :] * proj_ref[3, n_rows:, :]
        z0_ref[...] = z_meta[N_META - CONV_HIST:]

        def ext(j, lanes=slice(None)):
            if j < POOL_BUF:
                return sp_ref[j, :, lanes]
            return proj_ref[0, step_rows(j - POOL_BUF), lanes]

        d_steps = []
        for j in range(n_new):
            means = []
            for g, w in enumerate(POOL_WINDOWS):
                acc = ext(POOL_BUF + j, _lanes(g))
                for k in range(1, w):
                    acc = acc + ext(POOL_BUF + j - k, _lanes(g))
                means.append(acc * (1.0 / w))
            d_steps.append(jnp.concatenate(means, axis=-1) - ext(POOL_BUF + j))
        pw_bf_ref[...] = pw_ref[...].astype(jnp.bfloat16)
        ya = _pool_out(jnp.concatenate(d_steps, axis=0), pw_bf_ref, ps_ref)
        for j in range(POOL_BUF):
            npool_ref[j] = ext(n_new + j)

        for j in range(CONV_BUF):
            zext_ref[j] = sc_ref[:, j, :]
        for j in range(n_new):
            zext_ref[CONV_BUF + j] = proj_ref[2, step_rows(j), :] * proj_ref[3, step_rows(j), :]
        conv = [zext_ref[j] * cw_ref[0] + zext_ref[j + 1] * cw_ref[1]
                + zext_ref[j + 2] * cw_ref[2] for j in range(n_new)]
        yb = proj_ref[1, :n_rows, :] * jnp.concatenate(conv, axis=0)
        for j in range(CONV_BUF):
            nconv_ref[:, j, :] = zext_ref[n_new + j]

        wout_bf_ref[...] = wout_ref[...].astype(jnp.bfloat16)
        x_acc_ref[...] += _dot(jnp.concatenate([ya, yb], axis=-1), wout_bf_ref[...])
        hn2_ref[...] = _rmsnorm(x_acc_ref[...], g2_ref[...]).astype(jnp.bfloat16)

    @pl.when(i > MIX_STEP)
    def _():
        w1_bf_ref[...] = w1_ref[...].astype(jnp.bfloat16)
        w2_bf_ref[...] = w2_ref[...].astype(jnp.bfloat16)
        a = jnp.maximum(_dot(hn2_ref[...], w1_bf_ref[...]), 0.0)
        x_acc_ref[...] += _dot(a * a, w2_bf_ref[...])

    @pl.when(i == pl.num_programs(0) - 1)
    def _():
        x_acc_ref[...] = _rmsnorm(x_acc_ref[...], gf_ref[...])
        for j in range(n_new):
            y_ref[:, j, :] = x_acc_ref[step_rows(j), :]


def _resident(shape):
    zeros = (0,) * len(shape)
    return pl.BlockSpec(shape, lambda *_: zeros, pipeline_mode=pl.Buffered(1))


def kernel(x_prompt, x_sample, state_pool, state_conv, meta_tokens, norm1_g, w_in, pool_w,
           pool_scale, conv_w, w_out, norm2_g, w1, w2, final_g):
    f32, bf16 = jnp.float32, jnp.bfloat16
    batch, seq, _ = x_prompt.shape
    n_seq, n_new, _ = x_sample.shape
    n_rows = n_seq * n_new
    assert norm1_g.shape[0] == 1 and seq % TILE_T == 0 and seq >= POOL_HIST
    assert n_seq % SUBLANES == 0 and meta_tokens.shape[0] == N_META

    gf = final_g.reshape(1, D_MODEL)
    sp_steps = jnp.transpose(state_pool, (0, 2, 1, 3))
    cw = jnp.transpose(conv_w, (1, 0, 2))

    def ff_chunk(i):
        return jnp.clip(i - (MIX_STEP + 1), 0, N_FF_CHUNKS - 1)

    in_chunk_spec = pl.BlockSpec((None, D_MODEL, IN_CHUNK),
                                 lambda i: (0, 0, jnp.minimum(i, N_IN_CHUNKS - 1)))
    w1_chunk_spec = pl.BlockSpec((None, D_MODEL, S_FF_CHUNK), lambda i: (0, 0, ff_chunk(i)))
    w2_chunk_spec = pl.BlockSpec((None, S_FF_CHUNK, D_MODEL), lambda i: (0, ff_chunk(i), 0))

    (y_sample, npool_s, nconv_s, u_meta, z_meta,
     win_bf, pw_bf, wout_bf, w1_bf, w2_bf) = pl.pallas_call(
        _sample_kernel,
        grid=(N_IN_CHUNKS + 1 + N_FF_CHUNKS,),
        in_specs=[_resident(x_sample.shape), _resident(meta_tokens.shape),
                  _resident((None,) + sp_steps.shape[1:]),
                  _resident((None,) + state_conv.shape[1:]),
                  _resident(norm1_g.shape),
                  in_chunk_spec,
                  _resident((None,) + pool_w.shape[1:]),
                  _resident(pool_scale.shape),
                  _resident(cw.shape),
                  _resident((None,) + w_out.shape[1:]),
                  _resident(norm2_g.shape),
                  w1_chunk_spec, w2_chunk_spec,
                  _resident(gf.shape)],
        out_specs=(pl.BlockSpec(x_sample.shape, lambda i: (0, 0, 0)),
                   pl.BlockSpec((None,) + sp_steps.shape[1:], lambda i: (0, 0, 0, 0)),
                   pl.BlockSpec((None,) + state_conv.shape[1:], lambda i: (0, 0, 0, 0)),
                   pl.BlockSpec((POOL_HIST, POOL_W), lambda i: (0, 0)),
                   pl.BlockSpec((CONV_HIST, CONV_W), lambda i: (0, 0)),
                   in_chunk_spec,
                   pl.BlockSpec((None,) + pool_w.shape[1:], lambda i: (0, 0, 0, 0)),
                   pl.BlockSpec((None,) + w_out.shape[1:], lambda i: (0, 0, 0)),
                   w1_chunk_spec, w2_chunk_spec),
        out_shape=(jax.ShapeDtypeStruct(x_sample.shape, f32),
                   jax.ShapeDtypeStruct(sp_steps.shape, f32),
                   jax.ShapeDtypeStruct(state_conv.shape, f32),
                   jax.ShapeDtypeStruct((POOL_HIST, POOL_W), f32),
                   jax.ShapeDtypeStruct((CONV_HIST, CONV_W), f32),
                   jax.ShapeDtypeStruct(w_in.shape, bf16),
                   jax.ShapeDtypeStruct(pool_w.shape, bf16),
                   jax.ShapeDtypeStruct(w_out.shape, bf16),
                   jax.ShapeDtypeStruct(w1.shape, bf16),
                   jax.ShapeDtypeStruct(w2.shape, bf16)),
        scratch_shapes=[pltpu.VMEM((n_rows + N_META, D_MODEL), bf16),
                        pltpu.VMEM((N_IN_CHUNKS, n_rows + N_META, IN_CHUNK), f32),
                        pltpu.VMEM((n_rows, D_MODEL), f32),
                        pltpu.VMEM((n_rows, D_MODEL), bf16),
                        pltpu.VMEM((CONV_BUF + n_new, n_seq, CONV_W), f32)],
        compiler_params=pltpu.CompilerParams(
            dimension_semantics=("arbitrary",), vmem_limit_bytes=VMEM_LIMIT_BYTES),
        name="sample_layer",
    )(x_sample, meta_tokens, sp_steps, state_conv, norm1_g, w_in, pool_w, pool_scale,
      cw, w_out, norm2_g, w1, w2, gf)

    weights = (norm1_g, win_bf, pw_bf, pool_scale, cw, wout_bf, norm2_g, w1_bf, w2_bf, gf)
    weight_specs = [_resident((None,) + w.shape[1:] if w.ndim > 2 and w.shape[0] == 1
                              else w.shape) for w in weights]

    n_t = seq // TILE_T
    y_prompt, npool_p, nconv_p = pl.pallas_call(
        _prompt_kernel,
        grid=(batch, n_t),
        in_specs=[pl.BlockSpec((None, TILE_T, D_MODEL), lambda b, t: (b, t, 0)),
                  _resident((POOL_HIST, POOL_W)),
                  _resident((CONV_HIST, CONV_W))] + weight_specs,
        out_specs=(pl.BlockSpec((None, TILE_T, D_MODEL), lambda b, t: (b, t, 0)),
                   pl.BlockSpec((None, POOL_BUF, batch, POOL_W), lambda b, t: (0, 0, 0, 0)),
                   pl.BlockSpec((None, None, CONV_BUF, CONV_W), lambda b, t: (0, b, 0, 0))),
        out_shape=(jax.ShapeDtypeStruct((batch, seq, D_MODEL), f32),
                   jax.ShapeDtypeStruct((1, POOL_BUF, batch, POOL_W), f32),
                   jax.ShapeDtypeStruct((1, batch, CONV_BUF, CONV_W), f32)),
        scratch_shapes=[pltpu.VMEM((POOL_HIST, POOL_W), f32),
                        pltpu.VMEM((CONV_HIST, CONV_W), f32),
                        pltpu.VMEM((batch, POOL_HIST, POOL_W), f32)],
        compiler_params=pltpu.CompilerParams(
            dimension_semantics=("arbitrary", "arbitrary"),
            vmem_limit_bytes=VMEM_LIMIT_BYTES),
        name="prompt_layer",
    )(x_prompt, u_meta, z_meta, *weights)

    return (y_prompt, y_sample,
            jnp.transpose(npool_p, (0, 2, 1, 3)), nconv_p,
            jnp.transpose(npool_s, (0, 2, 1, 3)), nconv_s)
```
